```python
import jax, jax.numpy as jnp
from jax import lax
import numpy as np

D_MODEL = 1024
BATCH = 4
SEQ = 4096
DEPTH = 1
DEC_BATCH = 128
DEC_SEQ = 8
PAST_LEN = 8192
PAGE_SIZE = 128

MLA_HEADS = 8
Q_RANK = 384
KV_RANK = 256
NOPE_DIM = 128
ROPE_DIM = 64
V_DIM = 128
ROPE_THETA = 10000.0
Q_BLOCK = 128
RNN_WIDTH = D_MODEL
RNN_BLOCKS = 8
RNN_BLOCK_DIM = RNN_WIDTH // RNN_BLOCKS
CONV_WIDTH = 4
LRU_C = 8.0
MEM_TOKENS = 256
MEM_HEADS = 4
MEM_HEAD_DIM = D_MODEL // MEM_HEADS
D_FF = 2816
EPS = 1e-6
IN_SPLITS = (Q_RANK, KV_RANK, ROPE_DIM, RNN_WIDTH, RNN_WIDTH, D_MODEL, D_MODEL)
IN_WIDTH = Q_RANK + KV_RANK + ROPE_DIM + 2 * RNN_WIDTH + 2 * D_MODEL

kernel_name = 'hybrid_mla_rglru_macaron_decode_step'


def rmsnorm(x, g):
    xf = x.astype(jnp.float32)
    inv = lax.rsqrt(jnp.mean(xf * xf, axis=-1, keepdims=True) + EPS)
    return (xf * inv).astype(x.dtype) * g


def swiglu(x, wg, wu, wd):
    return (jax.nn.silu(x @ wg) * (x @ wu)) @ wd


def half_ffn(x, pre, post, wg, wu, wd):
    return x + 0.5 * rmsnorm(swiglu(rmsnorm(x, pre), wg, wu, wd), post)


def rope_tables(pos):
    inv = ROPE_THETA ** (-jnp.arange(0, ROPE_DIM, 2, dtype=jnp.float32) / ROPE_DIM)
    ang = pos.astype(jnp.float32)[:, None] * inv[None, :]
    return jnp.cos(ang), jnp.sin(ang)


def apply_rope(x, cos, sin):
    x1, x2 = jnp.split(x, 2, axis=-1)
    cos = cos.astype(x.dtype)
    sin = sin.astype(x.dtype)
    return jnp.concatenate([x1 * cos - x2 * sin, x2 * cos + x1 * sin], axis=-1)


def mixer_front(h, pos, w_in, q_norm, kv_norm, w_uq, w_uk):
    z = h @ w_in
    bounds = np.cumsum(IN_SPLITS)[:-1].tolist()
    cq, ckv, kr, rnn_x, rnn_gate, g_mla, g_rnn = jnp.split(z, bounds, axis=-1)
    b, t = h.shape[0], h.shape[1]
    cos, sin = rope_tables(pos)
    q = (rmsnorm(cq, q_norm) @ w_uq).reshape(b, t, MLA_HEADS, NOPE_DIM + ROPE_DIM)
    q_nope, q_rope = q[..., :NOPE_DIM], q[..., NOPE_DIM:]
    q_rope = apply_rope(q_rope, cos[:, None, :], sin[:, None, :])
    q_lat = jnp.einsum('bthn,rhn->bthr', q_nope, w_uk)
    c_kv = rmsnorm(ckv, kv_norm)
    k_rope = apply_rope(kr, cos, sin)
    return q_lat, q_rope, c_kv, k_rope, rnn_x, rnn_gate, g_mla, g_rnn


def mla_prompt_attention(q_lat, q_rope, c_kv, k_rope):
    scale = (NOPE_DIM + ROPE_DIM) ** -0.5
    b, s = q_lat.shape[0], q_lat.shape[1]
    nb = s // Q_BLOCK
    ql = q_lat.reshape(b, nb, Q_BLOCK, MLA_HEADS, KV_RANK).transpose(1, 0, 2, 3, 4)
    qr = q_rope.reshape(b, nb, Q_BLOCK, MLA_HEADS, ROPE_DIM).transpose(1, 0, 2, 3, 4)
    key_pos = jnp.arange(s)

    def block(args):
        qlb, qrb, i = args
        sc = (jnp.einsum('bthr,bsr->bhts', qlb, c_kv)
              + jnp.einsum('bthd,bsd->bhts', qrb, k_rope)).astype(jnp.float32) * scale
        qpos = i * Q_BLOCK + jnp.arange(Q_BLOCK)
        sc = jnp.where(key_pos[None, :] <= qpos[:, None], sc, -jnp.inf)
        p = jax.nn.softmax(sc, axis=-1).astype(c_kv.dtype)
        return jnp.einsum('bhts,bsr->bthr', p, c_kv)

    o = lax.map(block, (ql, qr, jnp.arange(nb)))
    return o.transpose(1, 0, 2, 3, 4).reshape(b, s, MLA_HEADS, KV_RANK)


def mla_sample_attention(q_lat, q_rope, c_new, kr_new, c_past, kr_past):
    scale = (NOPE_DIM + ROPE_DIM) ** -0.5
    t = q_lat.shape[1]
    p_len = c_past.shape[1]
    s_past = (jnp.einsum('bthr,bsr->bhts', q_lat, c_past)
              + jnp.einsum('bthd,bsd->bhts', q_rope, kr_past)).astype(jnp.float32)
    s_new = (jnp.einsum('bthr,bsr->bhts', q_lat, c_new)
             + jnp.einsum('bthd,bsd->bhts', q_rope, kr_new)).astype(jnp.float32)
    causal = jnp.arange(t)[None, :] <= jnp.arange(t)[:, None]
    s_new = jnp.where(causal, s_new, -jnp.inf)
    p = jax.nn.softmax(jnp.concatenate([s_past, s_new], axis=-1) * scale, axis=-1).astype(c_new.dtype)
    return (jnp.einsum('bhts,bsr->bthr', p[..., :p_len], c_past)
            + jnp.einsum('bhts,bsr->bthr', p[..., p_len:], c_new))


def causal_conv(x, buf, conv_w, conv_b):
    t = x.shape[1]
    xp = jnp.concatenate([buf, x], axis=1)
    y = conv_b
    for k in range(CONV_WIDTH):
        y = y + xp[:, k:k + t] * conv_w[k]
    return y, xp[:, -(CONV_WIDTH - 1):]


def block_diag(x, w, b):
    xb = x.reshape(x.shape[0], x.shape[1], RNN_BLOCKS, RNN_BLOCK_DIM)
    return jnp.einsum('btgi,gij->btgj', xb, w).reshape(x.shape) + b


def rg_lru(x, h0, w_rg, b_rg, w_ig, b_ig, lru_lambda):
    r = jax.nn.sigmoid(block_diag(x, w_rg, b_rg).astype(jnp.float32))
    i = jax.nn.sigmoid(block_diag(x, w_ig, b_ig).astype(jnp.float32))
    log_a = -LRU_C * r * jax.nn.softplus(-lru_lambda.astype(jnp.float32))
    a = jnp.exp(log_a)
    u = jnp.sqrt(-jnp.expm1(2.0 * log_a)) * (i * x.astype(jnp.float32))

    def step(h, au):
        a_t, u_t = au
        h = a_t * h + u_t
        return h, h

    h_last, hs = lax.scan(step, h0.astype(jnp.float32), (jnp.swapaxes(a, 0, 1), jnp.swapaxes(u, 0, 1)))
    return jnp.swapaxes(hs, 0, 1).astype(x.dtype), h_last.astype(h0.dtype)


def mixer_back(o_lat, rnn_y, rnn_gate, g_mla, g_rnn, w_uv, w_o_mla, w_o_rnn, w_out):
    v_heads = jnp.einsum('bthr,rhv->bthv', o_lat, w_uv)
    o_mla = v_heads.reshape(v_heads.shape[0], v_heads.shape[1], MLA_HEADS * V_DIM) @ w_o_mla
    o_rnn = (rnn_y * jax.nn.gelu(rnn_gate)) @ w_o_rnn
    return (jax.nn.sigmoid(g_mla) * o_mla + jax.nn.sigmoid(g_rnn) * o_rnn) @ w_out


def mem_kv(mem, mem_norm, w_mem_k, w_mem_v):
    m = rmsnorm(mem, mem_norm)
    b, n = mem.shape[0], mem.shape[1]
    k = (m @ w_mem_k).reshape(b, n, MEM_HEADS, MEM_HEAD_DIM)
    v = (m @ w_mem_v).reshape(b, n, MEM_HEADS, MEM_HEAD_DIM)
    return k, v


def mem_attend(h, k, v, w_mem_q, w_mem_o):
    b, t = h.shape[0], h.shape[1]
    q = (h @ w_mem_q).reshape(b, t, MEM_HEADS, MEM_HEAD_DIM)
    s = jnp.einsum('bthd,bmhd->bhtm', q, k).astype(jnp.float32) * (MEM_HEAD_DIM ** -0.5)
    p = jax.nn.softmax(s, axis=-1).astype(v.dtype)
    o = jnp.einsum('bhtm,bmhd->bthd', p, v).reshape(b, t, D_MODEL)
    return o @ w_mem_o


def run_layer(x_p, x_s, mem_p, c_lat, c_kr, page_table, conv_s, h_s, mem_k_s, mem_v_s,
              norms, w_ffn_gate, w_ffn_up, w_ffn_down, w_in, q_norm, kv_norm, w_uq, w_uk, w_uv,
              w_o_mla, conv_w, conv_b, w_rg, b_rg, w_ig, b_ig, lru_lambda, w_o_rnn, w_out,
              mem_norm, w_mem_q, w_mem_k, w_mem_v, w_mem_o):
    b = x_p.shape[0]
    db, n_pages = page_table.shape
    past = n_pages * PAGE_SIZE
    pos_p = jnp.arange(x_p.shape[1])
    pos_s = past + jnp.arange(x_s.shape[1])

    x_p = half_ffn(x_p, norms[0], norms[1], w_ffn_gate[0], w_ffn_up[0], w_ffn_down[0])
    x_s = half_ffn(x_s, norms[0], norms[1], w_ffn_gate[0], w_ffn_up[0], w_ffn_down[0])

    h = rmsnorm(x_p, norms[2])
    q_lat, q_rope, lat_p, kr_p, rx, rgate, gm, gr = mixer_front(h, pos_p, w_in, q_norm, kv_norm, w_uq, w_uk)
    o_lat = mla_prompt_attention(q_lat, q_rope, lat_p, kr_p)
    xc, conv_p_new = causal_conv(rx, jnp.zeros((b, CONV_WIDTH - 1, RNN_WIDTH), rx.dtype), conv_w, conv_b)
    ry, h_p_new = rg_lru(xc, jnp.zeros((b, RNN_WIDTH), rx.dtype), w_rg, b_rg, w_ig, b_ig, lru_lambda)
    x_p = x_p + rmsnorm(mixer_back(o_lat, ry, rgate, gm, gr, w_uv, w_o_mla, w_o_rnn, w_out), norms[3])

    h = rmsnorm(x_s, norms[2])
    q_lat, q_rope, lat_s, kr_s, rx, rgate, gm, gr = mixer_front(h, pos_s, w_in, q_norm, kv_norm, w_uq, w_uk)
    c_past = c_lat[page_table].reshape(db, past, KV_RANK)
    kr_past = c_kr[page_table].reshape(db, past, ROPE_DIM)
    o_lat = mla_sample_attention(q_lat, q_rope, lat_s, kr_s, c_past, kr_past)
    xc, conv_s_new = causal_conv(rx, conv_s, conv_w, conv_b)
    ry, h_s_new = rg_lru(xc, h_s, w_rg, b_rg, w_ig, b_ig, lru_lambda)
    x_s = x_s + rmsnorm(mixer_back(o_lat, ry, rgate, gm, gr, w_uv, w_o_mla, w_o_rnn, w_out), norms[3])

    mk_p, mv_p = mem_kv(mem_p, mem_norm, w_mem_k, w_mem_v)
    x_p = x_p + rmsnorm(mem_attend(rmsnorm(x_p, norms[4]), mk_p, mv_p, w_mem_q, w_mem_o), norms[5])
    x_s = x_s + rmsnorm(mem_attend(rmsnorm(x_s, norms[4]), mem_k_s, mem_v_s, w_mem_q, w_mem_o), norms[5])

    x_p = half_ffn(x_p, norms[6], norms[7], w_ffn_gate[1], w_ffn_up[1], w_ffn_down[1])
    x_s = half_ffn(x_s, norms[6], norms[7], w_ffn_gate[1], w_ffn_up[1], w_ffn_down[1])
    return (x_p, x_s, lat_p, kr_p, lat_s, kr_s, conv_p_new, conv_s_new, h_p_new, h_s_new, mk_p, mv_p)


def setup_inputs(seed: int = 0) -> dict:
    key = jax.random.key(seed)
    ks = iter(jax.random.split(key, 48))
    f32 = jnp.float32
    L = DEPTH

    def nrm(shape, scale):
        return jax.random.normal(next(ks), shape, f32) * scale

    n_pages = PAST_LEN // PAGE_SIZE
    n_used = DEC_BATCH * n_pages
    n_pool = n_used + n_used // 4
    x_prompt = nrm((BATCH, SEQ, D_MODEL), 1.0)
    x_sample = nrm((DEC_BATCH, DEC_SEQ, D_MODEL), 1.0)
    mem_prompt = nrm((BATCH, MEM_TOKENS, D_MODEL), 1.0)
    cache_mla_latent = nrm((L, n_pool, PAGE_SIZE, KV_RANK), 1.0)
    cache_mla_krope = nrm((L, n_pool, PAGE_SIZE, ROPE_DIM), 1.0)
    page_table = jax.random.permutation(next(ks), n_pool)[:n_used].reshape(DEC_BATCH, n_pages).astype(jnp.int32)
    state_rnn_conv = nrm((L, DEC_BATCH, CONV_WIDTH - 1, RNN_WIDTH), 1.0)
    state_rnn_h = nrm((L, DEC_BATCH, RNN_WIDTH), 0.5)
    cache_mem_k = nrm((L, DEC_BATCH, MEM_TOKENS, MEM_HEADS, MEM_HEAD_DIM), 1.0)
    cache_mem_v = nrm((L, DEC_BATCH, MEM_TOKENS, MEM_HEADS, MEM_HEAD_DIM), 1.0)

    norms = 1.0 + nrm((L, 8, D_MODEL), 0.05)
    w_ffn_gate = nrm((L, 2, D_MODEL, D_FF), D_MODEL ** -0.5)
    w_ffn_up = nrm((L, 2, D_MODEL, D_FF), D_MODEL ** -0.5)
    w_ffn_down = nrm((L, 2, D_FF, D_MODEL), D_FF ** -0.5)
    w_in = nrm((L, D_MODEL, IN_WIDTH), D_MODEL ** -0.5)
    q_norm = 1.0 + nrm((L, Q_RANK), 0.05)
    kv_norm = 1.0 + nrm((L, KV_RANK), 0.05)
    w_uq = nrm((L, Q_RANK, MLA_HEADS * (NOPE_DIM + ROPE_DIM)), Q_RANK ** -0.5)
    w_uk = nrm((L, KV_RANK, MLA_HEADS, NOPE_DIM), KV_RANK ** -0.5)
    w_uv = nrm((L, KV_RANK, MLA_HEADS, V_DIM), KV_RANK ** -0.5)
    w_o_mla = nrm((L, MLA_HEADS * V_DIM, D_MODEL), (MLA_HEADS * V_DIM) ** -0.5)
    conv_w = nrm((L, CONV_WIDTH, RNN_WIDTH), CONV_WIDTH ** -0.5)
    conv_b = nrm((L, RNN_WIDTH), 0.01)
    w_rg = nrm((L, RNN_BLOCKS, RNN_BLOCK_DIM, RNN_BLOCK_DIM), RNN_BLOCK_DIM ** -0.5)
    b_rg = nrm((L, RNN_WIDTH), 0.01)
    w_ig = nrm((L, RNN_BLOCKS, RNN_BLOCK_DIM, RNN_BLOCK_DIM), RNN_BLOCK_DIM ** -0.5)
    b_ig = nrm((L, RNN_WIDTH), 0.01)
    a_c = jax.random.uniform(next(ks), (L, RNN_WIDTH), f32, minval=0.9, maxval=0.999)
    a0 = a_c ** (1.0 / LRU_C)
    lru_lambda = jnp.log(a0) - jnp.log1p(-a0)
    w_o_rnn = nrm((L, RNN_WIDTH, D_MODEL), RNN_WIDTH ** -0.5)
    w_out = nrm((L, D_MODEL, D_MODEL), D_MODEL ** -0.5)
    mem_norm = 1.0 + nrm((L, D_MODEL), 0.05)
    w_mem_q = nrm((L, D_MODEL, D_MODEL), D_MODEL ** -0.5)
    w_mem_k = nrm((L, D_MODEL, D_MODEL), D_MODEL ** -0.5)
    w_mem_v = nrm((L, D_MODEL, D_MODEL), D_MODEL ** -0.5)
    w_mem_o = nrm((L, D_MODEL, D_MODEL), D_MODEL ** -0.5)
    return {'x_prompt': x_prompt, 'x_sample': x_sample, 'mem_prompt': mem_prompt,
            'cache_mla_latent': cache_mla_latent, 'cache_mla_krope': cache_mla_krope, 'page_table': page_table,
            'state_rnn_conv': state_rnn_conv, 'state_rnn_h': state_rnn_h,
            'cache_mem_k': cache_mem_k, 'cache_mem_v': cache_mem_v,
            'norms': norms, 'w_ffn_gate': w_ffn_gate, 'w_ffn_up': w_ffn_up, 'w_ffn_down': w_ffn_down,
            'w_in': w_in, 'q_norm': q_norm, 'kv_norm': kv_norm, 'w_uq': w_uq, 'w_uk': w_uk, 'w_uv': w_uv,
            'w_o_mla': w_o_mla, 'conv_w': conv_w, 'conv_b': conv_b, 'w_rg': w_rg, 'b_rg': b_rg,
            'w_ig': w_ig, 'b_ig': b_ig, 'lru_lambda': lru_lambda, 'w_o_rnn': w_o_rnn, 'w_out': w_out,
            'mem_norm': mem_norm, 'w_mem_q': w_mem_q, 'w_mem_k': w_mem_k, 'w_mem_v': w_mem_v, 'w_mem_o': w_mem_o}


def reference(x_prompt, x_sample, mem_prompt, cache_mla_latent, cache_mla_krope, page_table,
              state_rnn_conv, state_rnn_h, cache_mem_k, cache_mem_v,
              norms, w_ffn_gate, w_ffn_up, w_ffn_down, w_in, q_norm, kv_norm, w_uq, w_uk, w_uv,
              w_o_mla, conv_w, conv_b, w_rg, b_rg, w_ig, b_ig, lru_lambda, w_o_rnn, w_out,
              mem_norm, w_mem_q, w_mem_k, w_mem_v, w_mem_o):
    x_p, x_s = x_prompt, x_sample
    layer_states = []
    for l in range(DEPTH):
        x_p, x_s, *st = run_layer(
            x_p, x_s, mem_prompt, cache_mla_latent[l], cache_mla_krope[l], page_table,
            state_rnn_conv[l], state_rnn_h[l], cache_mem_k[l], cache_mem_v[l],
            norms[l], w_ffn_gate[l], w_ffn_up[l], w_ffn_down[l], w_in[l], q_norm[l], kv_norm[l],
            w_uq[l], w_uk[l], w_uv[l], w_o_mla[l], conv_w[l], conv_b[l], w_rg[l], b_rg[l],
            w_ig[l], b_ig[l], lru_lambda[l], w_o_rnn[l], w_out[l],
            mem_norm[l], w_mem_q[l], w_mem_k[l], w_mem_v[l], w_mem_o[l])
        layer_states.append(st)
    new = [jnp.stack(s, axis=0) for s in zip(*layer_states)]
    return (x_p, x_s, new[0], new[1], new[2], new[3], new[4], new[5], new[6], new[7], new[8], new[9])
```

```python
import functools

import jax
import jax.numpy as jnp
from jax import lax
from jax.experimental import pallas as pl
from jax.experimental.pallas import tpu as pltpu

EPS = 1e-6
LRU_C = 8.0
ROPE_THETA = 10000.0
BF = jnp.bfloat16
F32 = jnp.float32

V7X_VMEM_BYTES = 64 * 1024 * 1024
VMEM_LIMIT_BYTES = V7X_VMEM_BYTES - 8 * 1024 * 1024
LANES = 128
SUBLANES = 8


def _cparams(n_axes):
    return pltpu.CompilerParams(
        dimension_semantics=("arbitrary",) * n_axes, vmem_limit_bytes=VMEM_LIMIT_BYTES)


def _const(shape):
    zeros = (0,) * len(shape)
    return pl.BlockSpec(shape, lambda *_: zeros, pipeline_mode=pl.Buffered(1))


def _rms(x, g):
    inv = lax.rsqrt(jnp.mean(x * x, axis=-1, keepdims=True) + EPS)
    return x * inv * g


def _dot(a, b):
    return jnp.dot(a, b, preferred_element_type=F32)


def _dot_t(a, b):
    return lax.dot_general(a, b, (((1,), (1,)), ((), ())), preferred_element_type=F32)


def _row_tile(n, want):
    t = min(n, want)
    assert n % t == 0, (n, t)
    return t


def _ff_chunks(f):
    step = 1024
    return tuple((c, min(c + step, f)) for c in range(0, f, step))


def _ffn_body(*refs, chunks, n_pre, n_post, with_mem):
    if with_mem:
        x_ref, om_ref, wo_ref, norms_ref, wg_ref, wu_ref, wd_ref, o_ref = refs
    else:
        x_ref, norms_ref, wg_ref, wu_ref, wd_ref, o_ref = refs
    x = x_ref[...]
    if with_mem:
        x = x + _rms(_dot(om_ref[...], wo_ref[...]), norms_ref[5:6, :])
    h = _rms(x, norms_ref[n_pre:n_pre + 1, :]).astype(BF)
    y = None
    for c0, c1 in chunks:
        g = _dot(h, wg_ref[:, c0:c1])
        u = _dot(h, wu_ref[:, c0:c1])
        a = (g * jax.nn.sigmoid(g) * u).astype(BF)
        part = _dot(a, wd_ref[c0:c1, :])
        y = part if y is None else y + part
    o_ref[...] = x + 0.5 * _rms(y, norms_ref[n_post:n_post + 1, :])


def _ffn(x, norms, wg, wu, wd, n_pre, n_post, om=None, wo=None, tm=512):
    n, d = x.shape
    f = wg.shape[1]
    tm = _row_tile(n, tm)
    with_mem = om is not None
    row = pl.BlockSpec((tm, d), lambda i: (i, 0))
    in_specs = [row]
    args = [x]
    if with_mem:
        in_specs += [row, _const(wo.shape)]
        args += [om, wo]
    in_specs += [_const(norms.shape), _const(wg.shape), _const(wu.shape), _const(wd.shape)]
    args += [norms, wg, wu, wd]
    body = functools.partial(_ffn_body, chunks=_ff_chunks(f), n_pre=n_pre, n_post=n_post,
                             with_mem=with_mem)
    return pl.pallas_call(
        body, grid=(n // tm,), in_specs=in_specs, out_specs=row,
        out_shape=jax.ShapeDtypeStruct((n, d), F32),
        compiler_params=_cparams(1), name="ffn_mem" if with_mem else "ffn")(*args)


def _front_body(*refs, absorb, n_heads, q_rank, kv_rank, scale):
    if absorb:
        (x_ref, norms_ref, w_in_ref, qg_ref, kvg_ref, w_uq_ref, cs_ref, w_ukt_ref,
         ql_ref, qr_ref, lat_ref, kro_ref, rx_ref, rest_ref) = refs
    else:
        (x_ref, norms_ref, w_in_ref, qg_ref, kvg_ref, w_uq_ref, cs_ref, w_ukv_ref,
         qn_ref, qr_ref, kn_ref, krd_ref, v_ref, lat_ref, kro_ref, rx_ref, rest_ref) = refs
    hd = n_heads * LANES
    c_kr = q_rank + kv_rank
    c_rx = c_kr + LANES
    width = rx_ref.shape[1]

    h = _rms(x_ref[...], norms_ref[2:3, :]).astype(BF)
    za = _dot(h, w_in_ref[:, :c_rx])
    rx_ref[...] = _dot(h, w_in_ref[:, c_rx:c_rx + width])
    rest_ref[...] = _dot(h, w_in_ref[:, c_rx + width:])

    cs = cs_ref[...]
    y = za[:, c_kr:c_rx] * cs
    ksum = y + pltpu.roll(y, LANES // 2, axis=1)
    kro_ref[...] = ksum[:, :LANES // 2]
    lat = _rms(za[:, q_rank:c_kr], kvg_ref[...])
    lat_ref[...] = lat

    qc = _rms(za[:, :q_rank], qg_ref[...]).astype(BF)
    qn = _dot(qc, w_uq_ref[:, :hd]) * scale
    qb = _dot(qc, w_uq_ref[:, hd:])
    cs_h = jnp.concatenate([cs] * n_heads, axis=1)
    yq = qb * cs_h
    lane = lax.broadcasted_iota(jnp.int32, yq.shape, 1)
    qsum = yq + pltpu.roll(yq, hd - LANES // 2, axis=1)
    low_half = jnp.bitwise_and(lane, LANES - 1) < LANES // 2
    qr_ref[...] = (jnp.where(low_half, qsum, 0.0) * scale).astype(qr_ref.dtype)

    if absorb:
        qnb = qn.astype(BF)
        for hh in range(n_heads):
            ql = _dot(qnb[:, hh * LANES:(hh + 1) * LANES], w_ukt_ref[hh])
            ql_ref[:, hh * kv_rank:(hh + 1) * kv_rank] = ql.astype(ql_ref.dtype)
    else:
        qn_ref[...] = qn.astype(qn_ref.dtype)
        krd_ref[...] = ksum.astype(krd_ref.dtype)
        latb = lat.astype(BF)
        kn_ref[...] = _dot(latb, w_ukv_ref[:, :hd]).astype(kn_ref.dtype)
        v_ref[...] = _dot(latb, w_ukv_ref[:, hd:]).astype(v_ref.dtype)


def _front(x, norms, w_in_r, q_norm, kv_norm, w_uq_r, cs, w_extra, *, absorb, n_heads, scale,
           cs_period_tiles, tm=256):
    n, d = x.shape
    q_rank = q_norm.shape[1]
    kv_rank = kv_norm.shape[1]
    hd = n_heads * LANES
    width = d
    rest_w = w_in_r.shape[1] - (q_rank + kv_rank + LANES) - width
    tm = _row_tile(n, tm)

    def row(w):
        return pl.BlockSpec((tm, w), lambda i: (i, 0))

    cs_spec = pl.BlockSpec((tm, LANES), lambda i: (i % cs_period_tiles, 0))
    in_specs = [row(d), _const(norms.shape), _const(w_in_r.shape), _const(q_norm.shape),
                _const(kv_norm.shape), _const(w_uq_r.shape), cs_spec, _const(w_extra.shape)]
    tail_specs = [row(kv_rank), row(LANES // 2), row(width), row(rest_w)]
    tail_shapes = [jax.ShapeDtypeStruct((n, kv_rank), F32),
                   jax.ShapeDtypeStruct((n, LANES // 2), F32),
                   jax.ShapeDtypeStruct((n, width), F32),
                   jax.ShapeDtypeStruct((n, rest_w), F32)]
    if absorb:
        out_specs = [row(n_heads * kv_rank), row(hd)] + tail_specs
        out_shape = [jax.ShapeDtypeStruct((n, n_heads * kv_rank), BF),
                     jax.ShapeDtypeStruct((n, hd), BF)] + tail_shapes
    else:
        out_specs = [row(hd), row(hd), row(hd), row(LANES), row(hd)] + tail_specs
        out_shape = [jax.ShapeDtypeStruct((n, hd), BF), jax.ShapeDtypeStruct((n, hd), BF),
                     jax.ShapeDtypeStruct((n, hd), BF), jax.ShapeDtypeStruct((n, LANES), BF),
                     jax.ShapeDtypeStruct((n, hd), BF)] + tail_shapes
    body = functools.partial(_front_body, absorb=absorb, n_heads=n_heads, q_rank=q_rank,
                             kv_rank=kv_rank, scale=scale)
    return pl.pallas_call(
        body, grid=(n // tm,), in_specs=in_specs, out_specs=out_specs, out_shape=out_shape,
        compiler_params=_cparams(1), name="front_sample" if absorb else "front_prompt")(
            x, norms, w_in_r, q_norm, kv_norm, w_uq_r, cs, w_extra)


def _softmax_update(s, v, carry):
    m, l, acc = carry
    m_new = jnp.maximum(m, jnp.max(s, axis=-1, keepdims=True))
    alpha = jnp.exp(m - m_new)
    p = jnp.exp(s - m_new)
    l = alpha * l + jnp.sum(p, axis=-1, keepdims=True)
    acc = alpha * acc + _dot(p.astype(BF), v)
    return m_new, l, acc


def _pattn_body(qn_ref, qr_ref, kn_ref, kr_ref, v_ref, o_ref, *, tq):
    i = pl.program_id(2)
    qn = qn_ref[...]
    qr = qr_ref[...]

    def tile(j, carry, masked):
        off = pl.multiple_of(j * tq, tq)
        s = _dot_t(qn, kn_ref[pl.ds(off, tq), :]) + _dot_t(qr, kr_ref[pl.ds(off, tq), :])
        if masked:
            rows = lax.broadcasted_iota(jnp.int32, s.shape, 0)
            cols = lax.broadcasted_iota(jnp.int32, s.shape, 1)
            s = jnp.where(cols <= rows, s, -jnp.inf)
        return _softmax_update(s, v_ref[pl.ds(off, tq), :], carry)

    carry = (jnp.full((tq, 1), -jnp.inf, F32), jnp.zeros((tq, 1), F32),
             jnp.zeros((tq, v_ref.shape[1]), F32))
    carry = lax.fori_loop(0, i, lambda j, c: tile(j, c, False), carry)
    _, l, acc = tile(i, carry, True)
    o_ref[...] = (acc / l).astype(o_ref.dtype)


def _prompt_attention(qn, qr, kn, krd, v, *, n_heads, tq=512):
    b, s, hd = qn.shape
    tq = _row_tile(s, tq)
    q_spec = pl.BlockSpec((None, tq, LANES), lambda bb, hh, i: (bb, i, hh))
    k_spec = pl.BlockSpec((None, s, LANES), lambda bb, hh, i: (bb, 0, hh))
    kr_spec = pl.BlockSpec((None, s, LANES), lambda bb, hh, i: (bb, 0, 0))
    return pl.pallas_call(
        functools.partial(_pattn_body, tq=tq), grid=(b, n_heads, s // tq),
        in_specs=[q_spec, q_spec, k_spec, kr_spec, k_spec], out_specs=q_spec,
        out_shape=jax.ShapeDtypeStruct((b, s, hd), BF),
        compiler_params=_cparams(3), name="prompt_attn")(qn, qr, kn, krd, v)


def _sattn_body(pt_ref, ql_ref, qr_ref, latn_ref, krn_ref, *rest, pps, n_heads):
    lat_refs = rest[:pps]
    kr_refs = rest[pps:2 * pps]
    o_ref = rest[2 * pps]
    m_sc, l_sc, acc_sc = rest[2 * pps + 1:]
    del pt_ref
    j = pl.program_id(1)
    half = LANES // 2

    @pl.when(j == 0)
    def _():
        m_sc[...] = jnp.full(m_sc.shape, -jnp.inf, F32)
        l_sc[...] = jnp.zeros(l_sc.shape, F32)
        acc_sc[...] = jnp.zeros(acc_sc.shape, F32)

    ql = ql_ref[...]
    qr = qr_ref[:, :half]

    def update(s, v):
        m, l, acc = _softmax_update(s, v, (m_sc[...], l_sc[...], acc_sc[...]))
        m_sc[...] = m
        l_sc[...] = l
        acc_sc[...] = acc

    k = jnp.concatenate([r[...].astype(BF) for r in lat_refs], axis=0)
    kr = jnp.concatenate([r[...].astype(BF) for r in kr_refs], axis=0)
    update(_dot_t(ql, k) + _dot_t(qr, kr), k)

    @pl.when(j == pl.num_programs(1) - 1)
    def _():
        n_new, rank = latn_ref.shape
        pad = LANES - n_new
        kn = jnp.concatenate([latn_ref[...], jnp.zeros((pad, rank), F32)], axis=0).astype(BF)
        krn = jnp.concatenate([krn_ref[...], jnp.zeros((pad, half), F32)], axis=0).astype(BF)
        s = _dot_t(ql, kn) + _dot_t(qr, krn)
        assert n_heads & (n_heads - 1) == 0
        t_row = lax.shift_right_logical(lax.broadcasted_iota(jnp.int32, s.shape, 0),
                                        n_heads.bit_length() - 1)
        cols = lax.broadcasted_iota(jnp.int32, s.shape, 1)
        update(jnp.where(cols <= t_row, s, -jnp.inf), kn)
        o_ref[...] = (acc_sc[...] / l_sc[...]).astype(o_ref.dtype)


def _sample_attention(ql, qr, lat_new, kr_new, cache_lat, cache_kr, page_table, *, n_heads,
                      pages_per_step=8):
    db, rows, rank = ql.shape
    n_pages = page_table.shape[1]
    pps = min(pages_per_step, n_pages)
    assert n_pages % pps == 0
    page, rope = cache_kr.shape[2], cache_kr.shape[3]
    n_new = lat_new.shape[1]
    pt = page_table.reshape(-1)

    def cache_spec(width, k):
        return pl.BlockSpec(
            (None, None, page, width),
            lambda bb, j, pt_ref: (0, pt_ref[bb * n_pages + j * pps + k], 0, 0))

    def per_b(r, w):
        return pl.BlockSpec((None, r, w), lambda bb, j, pt_ref: (bb, 0, 0))

    in_specs = [per_b(rows, rank), per_b(rows, qr.shape[2]), per_b(n_new, rank),
                per_b(n_new, rope)]
    in_specs += [cache_spec(rank, k) for k in range(pps)]
    in_specs += [cache_spec(rope, k) for k in range(pps)]
    grid_spec = pltpu.PrefetchScalarGridSpec(
        num_scalar_prefetch=1, grid=(db, n_pages // pps), in_specs=in_specs,
        out_specs=per_b(rows, rank),
        scratch_shapes=[pltpu.VMEM((rows, 1), F32), pltpu.VMEM((rows, 1), F32),
                        pltpu.VMEM((rows, rank), F32)])
    return pl.pallas_call(
        functools.partial(_sattn_body, pps=pps, n_heads=n_heads), grid_spec=grid_spec,
        out_shape=jax.ShapeDtypeStruct((db, rows, rank), BF),
        compiler_params=_cparams(2), name="sample_attn")(
            pt, ql, qr, lat_new, kr_new, *([cache_lat] * pps), *([cache_kr] * pps))


def _rglru_core(x, prev8, h0, cw_ref, cb_ref, wg_ref, brg_ref, big_ref, lam_ref):
    g_n, t_n, c = x.shape
    conv_w = cw_ref.shape[0]
    n_blocks, bd, _ = wg_ref.shape
    t8 = lax.broadcasted_iota(jnp.int32, (g_n, SUBLANES, c), 1)

    xc = cb_ref[...] + cw_ref[conv_w - 1:conv_w, :] * x
    for d in range(1, conv_w):
        xs = pltpu.roll(x, d, axis=1)
        first = jnp.where(t8 < d, pltpu.roll(prev8, d, axis=1), xs[:, :SUBLANES, :])
        if t_n > SUBLANES:
            xs = jnp.concatenate([first, xs[:, SUBLANES:, :]], axis=1)
        else:
            xs = first
        xc = xc + cw_ref[conv_w - 1 - d:conv_w - d, :] * xs

    xc2 = xc.reshape(g_n * t_n, c)
    xb = xc2.astype(BF)
    pre = [_dot(xb[:, k * bd:(k + 1) * bd], wg_ref[k]) for k in range(n_blocks)]
    r = jax.nn.sigmoid(jnp.concatenate([p[:, :bd] for p in pre], axis=1) + brg_ref[...])
    gi = jax.nn.sigmoid(jnp.concatenate([p[:, bd:] for p in pre], axis=1) + big_ref[...])
    nl = -lam_ref[...]
    softplus = jnp.maximum(nl, 0.0) + jnp.log1p(jnp.exp(-jnp.abs(nl)))
    log_a = -LRU_C * r * softplus
    a = jnp.exp(log_a)
    th = jnp.tanh(log_a)
    one_minus_a2 = -2.0 * th / (1.0 - th)
    u = jnp.sqrt(one_minus_a2) * (gi * xc2)

    a = a.reshape(g_n, t_n, c)
    u = u.reshape(g_n, t_n, c)
    d = 1
    while d < t_n:
        if d < SUBLANES:
            tt = lax.broadcasted_iota(jnp.int32, a.shape, 1)
            valid = tt >= d
            u = jnp.where(valid, a * pltpu.roll(u, d, axis=1) + u, u)
            a = jnp.where(valid, a * pltpu.roll(a, d, axis=1), a)
        else:
            u = jnp.concatenate([u[:, :d], a[:, d:] * u[:, :t_n - d] + u[:, d:]], axis=1)
            a = jnp.concatenate([a[:, :d], a[:, d:] * a[:, :t_n - d]], axis=1)
        d *= 2
    return a * h0 + u


def _rglru_prompt_body(x_ref, cw_ref, cb_ref, wg_ref, brg_ref, big_ref, lam_ref,
                       y_ref, hout_ref, prev_sc, h_sc):
    @pl.when(pl.program_id(1) == 0)
    def _():
        prev_sc[...] = jnp.zeros(prev_sc.shape, F32)
        h_sc[...] = jnp.zeros(h_sc.shape, F32)

    x = x_ref[...][None]
    t_n = x.shape[1]
    hs = _rglru_core(x, prev_sc[...][None], h_sc[...][None], cw_ref, cb_ref, wg_ref,
                     brg_ref, big_ref, lam_ref)
    y_ref[...] = hs[0]
    h_last = hs[0, t_n - 1:t_n, :]
    h_sc[...] = h_last
    hout_ref[...] = h_last
    prev_sc[...] = x[0, t_n - SUBLANES:, :]


def _rglru_prompt(x, conv_w, conv_b, w_gates, b_rg, b_ig, lam, tt=256):
    b, s, c = x.shape
    tt = _row_tile(s, tt)
    consts = [conv_w, conv_b, w_gates, b_rg, b_ig, lam]
    return pl.pallas_call(
        _rglru_prompt_body, grid=(b, s // tt),
        in_specs=[pl.BlockSpec((None, tt, c), lambda bb, t: (bb, t, 0))]
        + [_const(w.shape) for w in consts],
        out_specs=[pl.BlockSpec((None, tt, c), lambda bb, t: (bb, t, 0)),
                   pl.BlockSpec((None, 1, c), lambda bb, t: (bb, 0, 0))],
        out_shape=[jax.ShapeDtypeStruct((b, s, c), F32), jax.ShapeDtypeStruct((b, 1, c), F32)],
        scratch_shapes=[pltpu.VMEM((SUBLANES, c), F32), pltpu.VMEM((1, c), F32)],
        compiler_params=_cparams(2), name="rglru_prompt")(x, *consts)


def _rglru_sample_body(x_ref, prev_ref, h0_ref, cw_ref, cb_ref, wg_ref, brg_ref, big_ref,
                       lam_ref, y_ref, hout_ref):
    x = x_ref[...]
    t_n = x.shape[1]
    hs = _rglru_core(x, prev_ref[...], h0_ref[...], cw_ref, cb_ref, wg_ref, brg_ref, big_ref,
                     lam_ref)
    y_ref[...] = hs
    hout_ref[...] = hs[:, t_n - 1:t_n, :]


def _rglru_sample(x, prev8, h0, conv_w, conv_b, w_gates, b_rg, b_ig, lam, bb=16):
    db, t, c = x.shape
    bb = _row_tile(db, bb)
    consts = [conv_w, conv_b, w_gates, b_rg, b_ig, lam]

    def blk(r):
        return pl.BlockSpec((bb, r, c), lambda i: (i, 0, 0))

    return pl.pallas_call(
        _rglru_sample_body, grid=(db // bb,),
        in_specs=[blk(t), blk(SUBLANES), blk(1)] + [_const(w.shape) for w in consts],
        out_specs=[blk(t), blk(1)],
        out_shape=[jax.ShapeDtypeStruct((db, t, c), F32), jax.ShapeDtypeStruct((db, 1, c), F32)],
        compiler_params=_cparams(1), name="rglru_sample")(x, prev8, h0, *consts)


def _back_body(*refs, absorb, n_heads, mem_scale):
    if absorb:
        (x_ref, o_ref, ry_ref, rest_ref, norms_ref, w_uv_ref, w_om_ref, w_or_ref, w_out_ref,
         w_mq_ref, xo_ref, qm_ref) = refs
    else:
        (x_ref, o_ref, ry_ref, rest_ref, norms_ref, w_om_ref, w_or_ref, w_out_ref,
         w_mq_ref, xo_ref, qm_ref) = refs
    c = ry_ref.shape[1]
    if absorb:
        rank = w_uv_ref.shape[1]
        vh = jnp.concatenate(
            [_dot(o_ref[:, hh * rank:(hh + 1) * rank], w_uv_ref[hh]) for hh in range(n_heads)],
            axis=1).astype(BF)
    else:
        vh = o_ref[...]
    o_mla = _dot(vh, w_om_ref[...])
    rgate = rest_ref[:, :c]
    o_rnn = _dot((ry_ref[...] * jax.nn.gelu(rgate)).astype(BF), w_or_ref[...])
    merged = (jax.nn.sigmoid(rest_ref[:, c:2 * c]) * o_mla
              + jax.nn.sigmoid(rest_ref[:, 2 * c:]) * o_rnn)
    y = _dot(merged.astype(BF), w_out_ref[...])
    x = x_ref[...] + _rms(y, norms_ref[3:4, :])
    xo_ref[...] = x
    qm = _dot(_rms(x, norms_ref[4:5, :]).astype(BF), w_mq_ref[...]) * mem_scale
    qm_ref[...] = qm.astype(qm_ref.dtype)


def _back(x, o, ry, rest, norms, w_uv_h, w_om, w_or, w_out, w_mq, *, absorb, n_heads, mem_scale,
          qm_dtype, tm=512):
    n, d = x.shape
    tm = _row_tile(n, tm)

    def row(w):
        return pl.BlockSpec((tm, w), lambda i: (i, 0))

    in_specs = [row(d), row(o.shape[1]), row(ry.shape[1]), row(rest.shape[1]), _const(norms.shape)]
    args = [x, o, ry, rest, norms]
    if absorb:
        in_specs.append(_const(w_uv_h.shape))
        args.append(w_uv_h)
    weights = [w_om, w_or, w_out, w_mq]
    in_specs += [_const(w.shape) for w in weights]
    args += weights
    body = functools.partial(_back_body, absorb=absorb, n_heads=n_heads, mem_scale=mem_scale)
    return pl.pallas_call(
        body, grid=(n // tm,), in_specs=in_specs, out_specs=[row(d), row(w_mq.shape[1])],
        out_shape=[jax.ShapeDtypeStruct((n, d), F32),
                   jax.ShapeDtypeStruct((n, w_mq.shape[1]), qm_dtype)],
        compiler_params=_cparams(1), name="back_sample" if absorb else "back_prompt")(*args)


def _memkv_body(m_ref, g_ref, wk_ref, wv_ref, k_ref, v_ref):
    m = _rms(m_ref[...], g_ref[...]).astype(BF)
    k_ref[...] = _dot(m, wk_ref[...])
    v_ref[...] = _dot(m, wv_ref[...])


def _mem_kv(mem, g, wk, wv, tm=512):
    n, d = mem.shape
    tm = _row_tile(n, tm)
    row = pl.BlockSpec((tm, d), lambda i: (i, 0))
    return pl.pallas_call(
        _memkv_body, grid=(n // tm,),
        in_specs=[row, _const(g.shape), _const(wk.shape), _const(wv.shape)],
        out_specs=[row, row],
        out_shape=[jax.ShapeDtypeStruct((n, d), F32)] * 2,
        compiler_params=_cparams(1), name="mem_kv")(mem, g, wk, wv)


def _memattn_body(q_ref, k_ref, v_ref, o_ref, *, n_heads):
    g_n = q_ref.shape[0]
    hd = q_ref.shape[2] // n_heads
    for g in range(g_n):
        outs = []
        for hh in range(n_heads):
            sl = slice(hh * hd, (hh + 1) * hd)
            q = q_ref[g, :, sl].astype(BF)
            s = _dot_t(q, k_ref[g, :, sl].astype(BF))
            p = jnp.exp(s - jnp.max(s, axis=-1, keepdims=True))
            o = _dot(p.astype(BF), v_ref[g, :, sl].astype(BF))
            outs.append(o / jnp.sum(p, axis=-1, keepdims=True))
        o_ref[g] = jnp.concatenate(outs, axis=1).astype(o_ref.dtype)


def _mem_attention(q, k, v, *, n_heads, groups, rows):
    b, t, d = q.shape
    m = k.shape[1]
    q_spec = pl.BlockSpec((groups, rows, d), lambda bb, i: (bb, i, 0))
    kv_spec = pl.BlockSpec((groups, m, d), lambda bb, i: (bb, 0, 0))
    return pl.pallas_call(
        functools.partial(_memattn_body, n_heads=n_heads), grid=(b // groups, t // rows),
        in_specs=[q_spec, kv_spec, kv_spec], out_specs=q_spec,
        out_shape=jax.ShapeDtypeStruct((b, t, d), BF),
        compiler_params=_cparams(2), name="mem_attn")(q, k, v)


def _rot_half_cols(w):
    half = w.shape[-1] // 2
    return jnp.concatenate([-w[..., half:], w[..., :half]], axis=-1)


def _rope_table(pos, rope_dim):
    inv = ROPE_THETA ** (-jnp.arange(0, rope_dim, 2, dtype=F32) / rope_dim)
    ang = pos.astype(F32)[:, None] * inv[None, :]
    cos, sin = jnp.cos(ang), jnp.sin(ang)
    return jnp.concatenate([cos, cos, sin, sin], axis=1)


def kernel(x_prompt, x_sample, mem_prompt, cache_mla_latent, cache_mla_krope, page_table,
           state_rnn_conv, state_rnn_h, cache_mem_k, cache_mem_v, norms, w_ffn_gate, w_ffn_up,
           w_ffn_down, w_in, q_norm, kv_norm, w_uq, w_uk, w_uv, w_o_mla, conv_w, conv_b, w_rg,
           b_rg, w_ig, b_ig, lru_lambda, w_o_rnn, w_out, mem_norm, w_mem_q, w_mem_k, w_mem_v,
           w_mem_o):
    depth = norms.shape[0]
    assert depth == 1, "single-layer step"
    b, s, d = x_prompt.shape
    db, ds_, _ = x_sample.shape
    q_rank = q_norm.shape[1]
    kv_rank, n_heads, nope = w_uk.shape[1:]
    rope = cache_mla_krope.shape[-1]
    c = conv_w.shape[-1]
    conv_width = conv_w.shape[1]
    mem_heads, mem_hd = cache_mem_k.shape[-2:]
    n_pages = page_table.shape[1]
    page = cache_mla_latent.shape[2]
    assert nope == LANES and 2 * rope == LANES and c == d and ds_ == SUBLANES
    scale = float(nope + rope) ** -0.5
    mem_scale = float(mem_hd) ** -0.5

    nrm = norms[0]
    wg = w_ffn_gate[0].astype(BF)
    wu = w_ffn_up[0].astype(BF)
    wd = w_ffn_down[0].astype(BF)
    wi = w_in[0]
    c_kr = q_rank + kv_rank
    w_kr = wi[:, c_kr:c_kr + rope]
    w_in_r = jnp.concatenate(
        [wi[:, :c_kr], w_kr, _rot_half_cols(w_kr), wi[:, c_kr + rope:]], axis=1).astype(BF)
    wq3 = w_uq[0].reshape(q_rank, n_heads, nope + rope)
    wq_rope = wq3[:, :, nope:]
    w_uq_r = jnp.concatenate(
        [wq3[:, :, :nope].reshape(q_rank, n_heads * nope),
         jnp.concatenate([wq_rope, _rot_half_cols(wq_rope)], axis=-1).reshape(q_rank, -1)],
        axis=1).astype(BF)
    w_ukv = jnp.concatenate([w_uk[0].reshape(kv_rank, -1), w_uv[0].reshape(kv_rank, -1)],
                            axis=1).astype(BF)
    w_ukt = jnp.transpose(w_uk[0], (1, 2, 0)).astype(BF)
    w_uv_h = jnp.transpose(w_uv[0], (1, 0, 2)).astype(BF)
    w_gates = jnp.concatenate([w_rg[0], w_ig[0]], axis=-1).astype(BF)
    w_om = w_o_mla[0].astype(BF)
    w_or = w_o_rnn[0].astype(BF)
    w_ou = w_out[0].astype(BF)
    w_mq = w_mem_q[0].astype(BF)
    w_mk = w_mem_k[0].astype(BF)
    w_mv = w_mem_v[0].astype(BF)
    w_mo = w_mem_o[0].astype(BF)
    qg, kvg, mg = q_norm, kv_norm, mem_norm
    cw, cb = conv_w[0], conv_b
    rnn_consts = (cw, cb, w_gates, b_rg, b_ig, lru_lambda)

    tm_front = 256
    cs_p = _rope_table(jnp.arange(s), rope)
    cs_s = jnp.tile(_rope_table(n_pages * page + jnp.arange(ds_), rope),
                    (min(tm_front, db * ds_) // ds_, 1))

    xp = x_prompt.reshape(b * s, d)
    xp = _ffn(xp, nrm, wg[0], wu[0], wd[0], 0, 1)
    (qn, qr, kn, krd, v, lat_p, kr_p, rx_p, rest_p) = _front(
        xp, nrm, w_in_r, qg, kvg, w_uq_r, cs_p, w_ukv, absorb=False, n_heads=n_heads,
        scale=scale, cs_period_tiles=s // min(tm_front, s), tm=tm_front)
    hd = n_heads * LANES
    o_p = _prompt_attention(qn.reshape(b, s, hd), qr.reshape(b, s, hd), kn.reshape(b, s, hd),
                            krd.reshape(b, s, LANES), v.reshape(b, s, hd), n_heads=n_heads)
    ry_p, h_p = _rglru_prompt(rx_p.reshape(b, s, c), *rnn_consts)
    xp, qm_p = _back(xp, o_p.reshape(b * s, hd), ry_p.reshape(b * s, c), rest_p, nrm, None,
                     w_om, w_or, w_ou, w_mq, absorb=False, n_heads=n_heads, mem_scale=mem_scale,
                     qm_dtype=BF)
    n_mem = mem_prompt.shape[1]
    mk_p, mv_p = _mem_kv(mem_prompt.reshape(b * n_mem, d), mg, w_mk, w_mv)
    om_p = _mem_attention(qm_p.reshape(b, s, d), mk_p.reshape(b, n_mem, d),
                          mv_p.reshape(b, n_mem, d), n_heads=mem_heads, groups=1,
                          rows=min(s, 512))
    y_p = _ffn(xp, nrm, wg[1], wu[1], wd[1], 6, 7, om=om_p.reshape(b * s, d), wo=w_mo)

    xs = x_sample.reshape(db * ds_, d)
    xs = _ffn(xs, nrm, wg[0], wu[0], wd[0], 0, 1)
    (ql, qr_s, lat_s, kr_s, rx_s, rest_s) = _front(
        xs, nrm, w_in_r, qg, kvg, w_uq_r, cs_s, w_ukt, absorb=True, n_heads=n_heads,
        scale=scale, cs_period_tiles=1, tm=tm_front)
    rows = ds_ * n_heads
    o_s = _sample_attention(
        ql.reshape(db, rows, kv_rank), qr_s.reshape(db, rows, LANES),
        lat_s.reshape(db, ds_, kv_rank), kr_s.reshape(db, ds_, rope),
        cache_mla_latent, cache_mla_krope, page_table, n_heads=n_heads)
    prev8 = jnp.pad(state_rnn_conv[0], ((0, 0), (SUBLANES - (conv_width - 1), 0), (0, 0)))
    ry_s, h_s = _rglru_sample(rx_s.reshape(db, ds_, c), prev8, state_rnn_h[0].reshape(db, 1, c),
                              *rnn_consts)
    xs, qm_s = _back(xs, o_s.reshape(db * ds_, n_heads * kv_rank), ry_s.reshape(db * ds_, c),
                     rest_s, nrm, w_uv_h, w_om, w_or, w_ou, w_mq, absorb=True, n_heads=n_heads,
                     mem_scale=mem_scale, qm_dtype=F32)
    om_s = _mem_attention(qm_s.reshape(db, ds_, d), cache_mem_k[0].reshape(db, n_mem, d),
                          cache_mem_v[0].reshape(db, n_mem, d), n_heads=mem_heads,
                          groups=min(db, 4), rows=ds_)
    y_s = _ffn(xs, nrm, wg[1], wu[1], wd[1], 6, 7, om=om_s.reshape(db * ds_, d), wo=w_mo)

    keep = conv_width - 1
    return (y_p.reshape(b, s, d), y_s.reshape(db, ds_, d),
            lat_p.reshape(1, b, s, kv_rank), kr_p.reshape(1, b, s, rope),
            lat_s.reshape(1, db, ds_, kv_rank), kr_s.reshape(1, db, ds_, rope),
            rx_p.reshape(b, s, c)[None, :, s - keep:, :],
            rx_s.reshape(db, ds_, c)[None, :, ds_ - keep:, :],
            h_p.reshape(1, b, c), h_s.reshape(1, db, c),
            mk_p.reshape(1, b, n_mem, mem_heads, mem_hd),
            mv_p.reshape(1, b, n_mem, mem_heads, mem_hd))
```
